```python
import math
import jax, jax.numpy as jnp
from jax import lax
import numpy as np

D_MODEL = 1024
BATCH = 32
SEQ = 256
DEPTH = 4
DEC_BATCH = 2
DEC_SEQ = 2048
PAST_LEN = 256

GRID_W = 64
N_MIXERS = 2
N_CONV_LAYERS = (DEPTH + 1) // 2
N_RET_LAYERS = DEPTH // 2
N_DENSE_LAYERS = (DEPTH + 1) // 2
N_MOE_LAYERS = DEPTH // 2
CONV_WIDTH = 31
RET_HEADS = 4
RET_DK = D_MODEL // RET_HEADS
RET_DV = 2 * RET_DK
QK_DIM = RET_HEADS * RET_DK
RET_VDIM = RET_HEADS * RET_DV
RET_IN_DIM = 2 * QK_DIM + 3 * RET_VDIM
RET_CHUNK = 128
ROPE_BASE = 10000.0
D_FF = 2816
N_EXPERTS = 8
TOP_K = 2
E_FF = 3584
EPS = 1e-6

kernel_name = "hybrid_conv_retention_diffusion_step"


def _rms_norm(x, g):
    xf = x.astype(jnp.float32)
    y = xf * lax.rsqrt(jnp.mean(xf * xf, axis=-1, keepdims=True) + EPS)
    return (y * g.astype(jnp.float32)).astype(x.dtype)


def _layer_norm(x, g, b):
    xf = x.astype(jnp.float32)
    mu = jnp.mean(xf, axis=-1, keepdims=True)
    var = jnp.mean(jnp.square(xf - mu), axis=-1, keepdims=True)
    y = (xf - mu) * lax.rsqrt(var + EPS) * g.astype(jnp.float32) + b.astype(jnp.float32)
    return y.astype(x.dtype)


def _dwconv(u, w, b):
    pad = CONV_WIDTH // 2
    y = lax.conv_general_dilated(u, w[:, None, :].astype(u.dtype), window_strides=(1,),
                                 padding=[(pad, pad)], dimension_numbers=('NWC', 'WIO', 'NWC'),
                                 feature_group_count=u.shape[-1])
    return y + b.astype(u.dtype)


def _conv_module(h, w_pw1, b_pw1, w_dw, b_dw, ln_g, ln_b, w_pw2, b_pw2, on_grid):
    a, g = jnp.split(h @ w_pw1 + b_pw1, 2, axis=-1)
    u = a * jax.nn.sigmoid(g)
    B, L, C = u.shape
    if on_grid:
        rows = L // GRID_W
        half = C // 2
        ur = _dwconv(u[..., :half].reshape(B * rows, GRID_W, half), w_dw[:, :half], b_dw[:half])
        ur = ur.reshape(B, L, half)
        uc = u[..., half:].reshape(B, rows, GRID_W, C - half).transpose(0, 2, 1, 3)
        uc = _dwconv(uc.reshape(B * GRID_W, rows, C - half), w_dw[:, half:], b_dw[half:])
        uc = uc.reshape(B, GRID_W, rows, C - half).transpose(0, 2, 1, 3).reshape(B, L, C - half)
        u = jnp.concatenate([ur, uc], axis=-1)
    else:
        u = _dwconv(u, w_dw, b_dw)
    u = jax.nn.silu(_layer_norm(u, ln_g, ln_b))
    return u @ w_pw2 + b_pw2


def _heads(t, d):
    B, L, _ = t.shape
    return t.reshape(B, L, RET_HEADS, d).transpose(0, 2, 1, 3)


def _rope_1d(x, pos):
    n = x.shape[-1] // 2
    inv = jnp.exp(-math.log(ROPE_BASE) * jnp.arange(n, dtype=jnp.float32) / n)
    ang = pos.astype(jnp.float32)[:, None] * inv[None, :]
    cos, sin = jnp.cos(ang), jnp.sin(ang)
    x1, x2 = x[..., :n], x[..., n:]
    return jnp.concatenate([x1 * cos - x2 * sin, x1 * sin + x2 * cos], axis=-1)


def _rope_axial(x, rows, cols):
    h = x.shape[-1] // 2
    return jnp.concatenate([_rope_1d(x[..., :h], rows), _rope_1d(x[..., h:], cols)], axis=-1)


def _head_norm(o):
    mu = jnp.mean(o, axis=-1, keepdims=True)
    var = jnp.mean(jnp.square(o - mu), axis=-1, keepdims=True)
    return (o - mu) * lax.rsqrt(var + EPS)


def _retention_chunkwise(q, k, v, log_g, s0):
    B, H, L, _ = q.shape
    n = L // RET_CHUNK
    idx = jnp.arange(RET_CHUNK, dtype=jnp.float32)
    rel = idx[:, None] - idx[None, :]
    lower = rel >= 0
    decay_mask = jnp.where(lower[None], jnp.exp(jnp.where(lower, rel, 0.0)[None] * log_g[:, None, None]), 0.0)
    q_decay = jnp.exp((idx + 1.0)[None, :] * log_g[:, None])[..., None]
    k_decay = jnp.exp((RET_CHUNK - 1.0 - idx)[None, :] * log_g[:, None])[..., None]
    chunk_decay = jnp.exp(RET_CHUNK * log_g)[:, None, None]

    def to_chunks(t):
        return t.reshape(B, H, n, RET_CHUNK, t.shape[-1]).transpose(2, 0, 1, 3, 4)

    def step(s, blk):
        qb, kb, vb = blk
        scores = jnp.einsum('bhid,bhjd->bhij', qb, kb) * decay_mask
        o = jnp.einsum('bhij,bhje->bhie', scores, vb) + jnp.einsum('bhid,bhde->bhie', qb * q_decay, s)
        s = s * chunk_decay + jnp.einsum('bhjd,bhje->bhde', kb * k_decay, vb)
        return s, o

    s_fin, o = lax.scan(step, s0, (to_chunks(q), to_chunks(k), to_chunks(v)))
    o = o.transpose(1, 2, 0, 3, 4).reshape(B, H, L, v.shape[-1])
    return o, s_fin


def _retention_mixer(h, w_in, decay_logit, w_out, s0, positions):
    B, L, _ = h.shape
    q, k, v, g_f, g_b = jnp.split(h @ w_in, [QK_DIM, 2 * QK_DIM, 2 * QK_DIM + RET_VDIM,
                                          2 * QK_DIM + 2 * RET_VDIM], axis=-1)
    q = _heads(q, RET_DK).astype(jnp.float32)
    k = _heads(k, RET_DK).astype(jnp.float32) * (RET_DK ** -0.5)
    v = _heads(v, RET_DV).astype(jnp.float32)
    if positions is not None:
        rows, cols = positions
        q = _rope_axial(q, rows, cols)
        k = _rope_axial(k, rows, cols)
    log_g = jax.nn.log_sigmoid(decay_logit.astype(jnp.float32))
    o_f, s_f = _retention_chunkwise(q, k, v, log_g[0], s0[:, 0])
    o_b, s_b = _retention_chunkwise(q[:, :, ::-1], k[:, :, ::-1], v[:, :, ::-1], log_g[1], s0[:, 1])
    o_b = o_b[:, :, ::-1]

    def merge(o):
        return _head_norm(o).transpose(0, 2, 1, 3).reshape(B, L, RET_VDIM)

    y = merge(o_f) * jax.nn.silu(g_f.astype(jnp.float32)) + merge(o_b) * jax.nn.silu(g_b.astype(jnp.float32))
    return y.astype(h.dtype) @ w_out, jnp.stack([s_f, s_b], axis=1)


def _swiglu(h, w_gu, w_down):
    g, u = jnp.split(h @ w_gu, 2, axis=-1)
    return (jax.nn.silu(g) * u) @ w_down


def _moe_swiglu(h, w_router, w_gu, w_down):
    B, L, D = h.shape
    t = h.reshape(B * L, D)
    logits = (t @ w_router).astype(jnp.float32)
    top_v, top_i = lax.top_k(logits, TOP_K)
    wts = jax.nn.softmax(top_v, axis=-1)
    comb = jnp.einsum('tk,tke->te', wts, jax.nn.one_hot(top_i, N_EXPERTS, dtype=jnp.float32)).astype(h.dtype)
    y = jnp.zeros_like(t)
    for e in range(N_EXPERTS):
        y = y + comb[:, e:e + 1] * _swiglu(t, w_gu[e], w_down[e])
    return y.reshape(B, L, D)


def _trunk(x, cond, ret_state, on_grid, w_ada, b_ada, norm_mix, norm_ffn, norm_final,
           conv_w_pw1, conv_b_pw1, conv_w_dw, conv_b_dw, conv_ln_g, conv_ln_b, conv_w_pw2, conv_b_pw2,
           ret_w_in, ret_decay_logit, ret_w_out, ffn_w_gu, ffn_w_down, moe_w_router, moe_w_gu, moe_w_down):
    B, L, _ = x.shape
    positions = None
    if on_grid:
        t = jnp.arange(L)
        positions = (t // GRID_W, t % GRID_W)
    ctx_states = []
    for i in range(DEPTH):
        j = i // N_MIXERS
        mod = jax.nn.silu(cond.astype(jnp.float32)) @ w_ada[i].astype(jnp.float32) + b_ada[i].astype(jnp.float32)
        sh1, sc1, g1, sh2, sc2, g2 = [m[:, None, :].astype(x.dtype) for m in jnp.split(mod, 6, axis=-1)]
        h = _rms_norm(x, norm_mix[i]) * (1 + sc1) + sh1
        if i % N_MIXERS == 0:
            out = _conv_module(h, conv_w_pw1[j], conv_b_pw1[j], conv_w_dw[j], conv_b_dw[j],
                               conv_ln_g[j], conv_ln_b[j], conv_w_pw2[j], conv_b_pw2[j], on_grid)
        else:
            if on_grid:
                s0 = ret_state[:, j].astype(jnp.float32)
            else:
                s0 = jnp.zeros((B, 2, RET_HEADS, RET_DK, RET_DV), jnp.float32)
            out, s_fin = _retention_mixer(h, ret_w_in[j], ret_decay_logit[j], ret_w_out[j], s0, positions)
            if not on_grid:
                ctx_states.append(s_fin)
        x = x + g1 * out
        h = _rms_norm(x, norm_ffn[i]) * (1 + sc2) + sh2
        if i % 2 == 0:
            f = _swiglu(h, ffn_w_gu[i // 2], ffn_w_down[i // 2])
        else:
            f = _moe_swiglu(h, moe_w_router[i // 2], moe_w_gu[i // 2], moe_w_down[i // 2])
        x = x + g2 * f
    return _rms_norm(x, norm_final), ctx_states


def setup_inputs(seed: int = 0) -> dict:
    key = jax.random.key(seed)
    ks = jax.random.split(key, 26)
    nrm = jax.random.normal
    f32 = jnp.float32
    D = D_MODEL
    decay_base = jnp.log(2.0 ** (5.0 + jnp.arange(RET_HEADS, dtype=f32)) - 1.0)
    return {
        "x_prompt": nrm(ks[0], (BATCH, SEQ, D), f32),
        "x_sample": nrm(ks[1], (DEC_BATCH, DEC_SEQ, D), f32),
        "state_ret": 0.5 * nrm(ks[2], (DEC_BATCH, N_RET_LAYERS, 2, RET_HEADS, RET_DK, RET_DV), f32),
        "c": nrm(ks[3], (DEC_BATCH, D), f32),
        "c_ctx": nrm(ks[4], (D,), f32),
        "w_ada": (0.5 * D ** -0.5) * nrm(ks[5], (DEPTH, D, 6 * D), f32),
        "b_ada": 0.02 * nrm(ks[6], (DEPTH, 6 * D), f32),
        "norm_mix": 1.0 + 0.02 * nrm(ks[7], (DEPTH, D), f32),
        "norm_ffn": 1.0 + 0.02 * nrm(ks[8], (DEPTH, D), f32),
        "norm_final": 1.0 + 0.02 * nrm(ks[9], (D,), f32),
        "conv_w_pw1": D ** -0.5 * nrm(ks[10], (N_CONV_LAYERS, D, 2 * D), f32),
        "conv_b_pw1": 0.02 * nrm(ks[11], (N_CONV_LAYERS, 2 * D), f32),
        "conv_w_dw": CONV_WIDTH ** -0.5 * nrm(ks[12], (N_CONV_LAYERS, CONV_WIDTH, D), f32),
        "conv_b_dw": 0.02 * nrm(ks[13], (N_CONV_LAYERS, D), f32),
        "conv_ln_g": 1.0 + 0.02 * nrm(ks[14], (N_CONV_LAYERS, D), f32),
        "conv_ln_b": 0.02 * nrm(ks[15], (N_CONV_LAYERS, D), f32),
        "conv_w_pw2": D ** -0.5 * nrm(ks[16], (N_CONV_LAYERS, D, D), f32),
        "conv_b_pw2": 0.02 * nrm(ks[17], (N_CONV_LAYERS, D), f32),
        "ret_w_in": D ** -0.5 * nrm(ks[18], (N_RET_LAYERS, D, RET_IN_DIM), f32),
        "ret_decay_logit": decay_base[None, None, :] + 0.05 * nrm(ks[19], (N_RET_LAYERS, 2, RET_HEADS), f32),
        "ret_w_out": RET_VDIM ** -0.5 * nrm(ks[20], (N_RET_LAYERS, RET_VDIM, D), f32),
        "ffn_w_gu": D ** -0.5 * nrm(ks[21], (N_DENSE_LAYERS, D, 2 * D_FF), f32),
        "ffn_w_down": D_FF ** -0.5 * nrm(ks[22], (N_DENSE_LAYERS, D_FF, D), f32),
        "moe_w_router": D ** -0.5 * nrm(ks[23], (N_MOE_LAYERS, D, N_EXPERTS), f32),
        "moe_w_gu": D ** -0.5 * nrm(ks[24], (N_MOE_LAYERS, N_EXPERTS, D, 2 * E_FF), f32),
        "moe_w_down": E_FF ** -0.5 * nrm(ks[25], (N_MOE_LAYERS, N_EXPERTS, E_FF, D), f32),
    }


def reference(x_prompt, x_sample, state_ret, c, c_ctx, w_ada, b_ada, norm_mix, norm_ffn, norm_final,
              conv_w_pw1, conv_b_pw1, conv_w_dw, conv_b_dw, conv_ln_g, conv_ln_b, conv_w_pw2, conv_b_pw2,
              ret_w_in, ret_decay_logit, ret_w_out, ffn_w_gu, ffn_w_down, moe_w_router, moe_w_gu, moe_w_down):
    y_prompt, ctx_states = _trunk(x_prompt, c_ctx[None, :], None, False, w_ada, b_ada, norm_mix, norm_ffn, norm_final,
                                  conv_w_pw1, conv_b_pw1, conv_w_dw, conv_b_dw, conv_ln_g, conv_ln_b, conv_w_pw2, conv_b_pw2,
                                  ret_w_in, ret_decay_logit, ret_w_out, ffn_w_gu, ffn_w_down, moe_w_router, moe_w_gu, moe_w_down)
    y_sample, _ = _trunk(x_sample, c, state_ret, True, w_ada, b_ada, norm_mix, norm_ffn, norm_final,
                         conv_w_pw1, conv_b_pw1, conv_w_dw, conv_b_dw, conv_ln_g, conv_ln_b, conv_w_pw2, conv_b_pw2,
                         ret_w_in, ret_decay_logit, ret_w_out, ffn_w_gu, ffn_w_down, moe_w_router, moe_w_gu, moe_w_down)
    state_ret_new = jnp.stack(ctx_states, axis=1).astype(x_prompt.dtype)
    return (y_prompt, y_sample, state_ret_new)
```

```python
import functools
import math

import jax
import jax.numpy as jnp
from jax import lax
from jax.experimental import pallas as pl
from jax.experimental.pallas import tpu as pltpu

F32 = jnp.float32
BF16 = jnp.bfloat16
I32 = jnp.int32
SDS = jax.ShapeDtypeStruct

D = 1024
B_CTX, L_CTX = 32, 256
B_SMP, L_SMP = 2, 2048
T_CTX = B_CTX * L_CTX
T_SMP = B_SMP * L_SMP
T = T_CTX + T_SMP
GRID_W = 64
GRID_H = L_SMP // GRID_W
DEPTH = 4
CONV_WIDTH = 31
CONV_PAD = CONV_WIDTH // 2
HEADS = 4
DK = 256
DV = 512
QK_DIM = HEADS * DK
VDIM = HEADS * DV
RET_IN = 2 * QK_DIM + 3 * VDIM
RET_CHUNK = 128
ROPE_BASE = 10000.0
D_FF = 2816
N_EXPERTS = 8
E_FF = 3584
EPS = 1e-6
LANES = 128
N_COND = 8
VMEM_LIMIT = 56 * 1024 * 1024

TM_MOE = 512
NT_MOE = (2 * T) // TM_MOE + N_EXPERTS
N_SLOTS = NT_MOE * TM_MOE


def _cparams(n_axes):
    return pltpu.CompilerParams(dimension_semantics=("arbitrary",) * n_axes,
                                vmem_limit_bytes=VMEM_LIMIT)


def _cond_row(i, tm):
    return jnp.maximum(0, (i * tm - T_CTX) // L_SMP + 1)


def _silu(x):
    return x * jax.nn.sigmoid(x)


def _rms(x, g):
    return x * lax.rsqrt(jnp.mean(x * x, axis=-1, keepdims=True) + EPS) * g


def _ada_kernel(c_ref, w_ref, b_ref, o_ref):
    s = _silu(c_ref[...]).astype(BF16)
    o_ref[...] = jnp.dot(s, w_ref[...].astype(BF16), preferred_element_type=F32) + b_ref[...]


def _ada(cond, w_ada, b_ada):
    tn = 1536
    n = 6 * D
    return pl.pallas_call(
        _ada_kernel, out_shape=SDS((DEPTH, N_COND, n), F32), grid=(DEPTH, n // tn),
        in_specs=[pl.BlockSpec((N_COND, D), lambda l, j: (0, 0)),
                  pl.BlockSpec((None, D, tn), lambda l, j: (l, 0, j)),
                  pl.BlockSpec((None, 1, tn), lambda l, j: (l, 0, j))],
        out_specs=pl.BlockSpec((None, N_COND, tn), lambda l, j: (l, 0, j)),
        compiler_params=_cparams(2), name="ada_mod",
    )(cond, w_ada, b_ada.reshape(DEPTH, 1, n))


def _norm0_kernel(xc_ref, xs_ref, g_ref, m_ref, xo_ref, h_ref, *, n_ctx):
    i = pl.program_id(0)

    def emit(x_ref):
        x = x_ref[...]
        xo_ref[...] = x
        m = m_ref[...]
        h_ref[...] = (_rms(x, g_ref[...]) * (1.0 + m[1:2]) + m[0:1]).astype(h_ref.dtype)

    pl.when(i < n_ctx)(lambda: emit(xc_ref))
    pl.when(i >= n_ctx)(lambda: emit(xs_ref))


def _norm0(xc, xs, g, mods):
    tm = 512
    n_ctx = T_CTX // tm
    return pl.pallas_call(
        functools.partial(_norm0_kernel, n_ctx=n_ctx),
        out_shape=(SDS((T, D), F32), SDS((T, D), BF16)), grid=(T // tm,),
        in_specs=[pl.BlockSpec((tm, D), lambda i: (jnp.minimum(i, n_ctx - 1), 0)),
                  pl.BlockSpec((tm, D), lambda i: (jnp.maximum(i - n_ctx, 0), 0)),
                  pl.BlockSpec((1, D), lambda i: (0, 0)),
                  pl.BlockSpec((None, None, 6, D), lambda i: (0, _cond_row(i, tm), 0, 0))],
        out_specs=(pl.BlockSpec((tm, D), lambda i: (i, 0)), pl.BlockSpec((tm, D), lambda i: (i, 0))),
        compiler_params=_cparams(1), name="norm0",
    )(xc, xs, g, mods)


def _mm_pair_kernel(*refs, act, has_bias):
    if has_bias:
        a_ref, w1_ref, w2_ref, b1_ref, b2_ref, o_ref, w1s, w2s = refs
    else:
        a_ref, w1_ref, w2_ref, o_ref, w1s, w2s = refs

    @pl.when(pl.program_id(1) == 0)
    def _():
        w1s[...] = w1_ref[...].astype(BF16)
        w2s[...] = w2_ref[...].astype(BF16)

    a = a_ref[...]
    p = jnp.dot(a, w1s[...], preferred_element_type=F32)
    q = jnp.dot(a, w2s[...], preferred_element_type=F32)
    if has_bias:
        p = p + b1_ref[...]
        q = q + b2_ref[...]
    r = p * jax.nn.sigmoid(q) if act == "glu" else _silu(p) * q
    o_ref[...] = r.astype(o_ref.dtype)


def _mm_pair(a, w, layer, bias, n_half, tn, act, out_dtype):
    tm = 1024
    nj = n_half // tn
    in_specs = [pl.BlockSpec((tm, D), lambda j, i: (i, 0)),
                pl.BlockSpec((None, D, tn), lambda j, i: (layer, 0, j)),
                pl.BlockSpec((None, D, tn), lambda j, i: (layer, 0, j + nj))]
    args = [a, w, w]
    if bias is not None:
        b3 = bias.reshape(bias.shape[0], 1, 2 * n_half)
        in_specs += [pl.BlockSpec((None, 1, tn), lambda j, i: (layer, 0, j)),
                     pl.BlockSpec((None, 1, tn), lambda j, i: (layer, 0, j + nj))]
        args += [b3, b3]
    return pl.pallas_call(
        functools.partial(_mm_pair_kernel, act=act, has_bias=bias is not None),
        out_shape=SDS((T, n_half), out_dtype), grid=(nj, T // tm),
        in_specs=in_specs, out_specs=pl.BlockSpec((tm, tn), lambda j, i: (i, j)),
        scratch_shapes=[pltpu.VMEM((D, tn), BF16), pltpu.VMEM((D, tn), BF16)],
        compiler_params=_cparams(2), name="mm_" + act,
    )(*args)


def _mm_kernel(a_ref, w_ref, o_ref, ws):
    @pl.when(pl.program_id(1) == 0)
    def _():
        ws[...] = w_ref[...].astype(BF16)

    o_ref[...] = jnp.dot(a_ref[...], ws[...], preferred_element_type=F32).astype(o_ref.dtype)


def _mm(a, w, layer, n, out_dtype):
    tm, tn = 1024, 1024
    return pl.pallas_call(
        _mm_kernel, out_shape=SDS((T, n), out_dtype), grid=(n // tn, T // tm),
        in_specs=[pl.BlockSpec((tm, D), lambda j, i: (i, 0)),
                  pl.BlockSpec((None, D, tn), lambda j, i: (layer, 0, j))],
        out_specs=pl.BlockSpec((tm, tn), lambda j, i: (i, j)),
        scratch_shapes=[pltpu.VMEM((D, tn), BF16)],
        compiler_params=_cparams(2), name="mm_ret_in",
    )(a, w)


CONV_LB = 512
CONV_RC = 64
CONV_OFF = 16


def _conv_ctx_kernel(u_ref, w_ref, b_ref, o_ref, pad):
    zeros = jnp.zeros((CONV_OFF, CONV_LB), F32)
    pad[0:CONV_OFF, :] = zeros
    pad[CONV_OFF + L_CTX:CONV_OFF + L_CTX + CONV_OFF, :] = zeros
    pad[CONV_OFF:CONV_OFF + L_CTX, :] = u_ref[...]
    for lb in range(CONV_LB // LANES):
        ls = slice(lb * LANES, (lb + 1) * LANES)
        for rc in range(L_CTX // CONV_RC):
            acc = jnp.zeros((CONV_RC, LANES), F32)
            for k in range(CONV_WIDTH):
                start = rc * CONV_RC + CONV_OFF - CONV_PAD + k
                acc = acc + w_ref[k:k + 1, ls] * pad[start:start + CONV_RC, ls]
            o_ref[rc * CONV_RC:(rc + 1) * CONV_RC, ls] = acc + b_ref[:, ls]


def _conv_ctx(u, w, b, layer):
    return pl.pallas_call(
        _conv_ctx_kernel, out_shape=SDS((T_CTX, D), F32), grid=(B_CTX, D // CONV_LB),
        in_specs=[pl.BlockSpec((L_CTX, CONV_LB), lambda b, j: (b, j)),
                  pl.BlockSpec((None, CONV_WIDTH, CONV_LB), lambda b, j: (layer, 0, j)),
                  pl.BlockSpec((None, 1, CONV_LB), lambda b, j: (layer, 0, j))],
        out_specs=pl.BlockSpec((L_CTX, CONV_LB), lambda b, j: (b, j)),
        scratch_shapes=[pltpu.VMEM((L_CTX + 2 * CONV_OFF, CONV_LB), F32)],
        compiler_params=_cparams(2), name="conv_ctx",
    )(u, w, b.reshape(b.shape[0], 1, D))


def _conv_smp_kernel(u_ref, w_ref, b_ref, o_ref, pad_r, pad_c):
    j = pl.program_id(1)

    @pl.when(j < (D // 2) // LANES)
    def _():
        zeros = jnp.zeros((GRID_H, CONV_OFF, LANES), F32)
        pad_r[:, 0:CONV_OFF, :] = zeros
        pad_r[:, CONV_OFF + GRID_W:, :] = zeros
        pad_r[:, CONV_OFF:CONV_OFF + GRID_W, :] = u_ref[...]

        def body(r, c):
            acc = jnp.zeros((GRID_W, LANES), F32)
            for k in range(CONV_WIDTH):
                acc = acc + w_ref[k:k + 1, :] * pad_r[r, pl.ds(CONV_OFF - CONV_PAD + k, GRID_W), :]
            o_ref[r] = acc + b_ref[...]
            return c

        lax.fori_loop(0, GRID_H, body, 0)

    @pl.when(j >= (D // 2) // LANES)
    def _():
        zeros = jnp.zeros((CONV_PAD, GRID_W, LANES), F32)
        pad_c[0:CONV_PAD] = zeros
        pad_c[CONV_PAD + GRID_H:] = zeros
        pad_c[CONV_PAD:CONV_PAD + GRID_H] = u_ref[...]

        def body(r, c):
            acc = jnp.zeros((GRID_W, LANES), F32)
            for k in range(CONV_WIDTH):
                acc = acc + w_ref[k:k + 1, :] * pad_c[r + k]
            o_ref[r] = acc + b_ref[...]
            return c

        lax.fori_loop(0, GRID_H, body, 0)


def _conv_smp(u, w, b, layer):
    u3 = u.reshape(T // GRID_W, GRID_W, D)
    first = T_CTX // GRID_W // GRID_H
    out = pl.pallas_call(
        _conv_smp_kernel, out_shape=SDS((T_SMP // GRID_W, GRID_W, D), F32), grid=(B_SMP, D // LANES),
        in_specs=[pl.BlockSpec((GRID_H, GRID_W, LANES), lambda b, j: (first + b, 0, j)),
                  pl.BlockSpec((None, CONV_WIDTH, LANES), lambda b, j: (layer, 0, j)),
                  pl.BlockSpec((None, 1, LANES), lambda b, j: (layer, 0, j))],
        out_specs=pl.BlockSpec((GRID_H, GRID_W, LANES), lambda b, j: (b, 0, j)),
        scratch_shapes=[pltpu.VMEM((GRID_H, GRID_W + 2 * CONV_OFF, LANES), F32),
                        pltpu.VMEM((GRID_H + 2 * CONV_PAD, GRID_W, LANES), F32)],
        compiler_params=_cparams(2), name="conv_smp",
    )(u3, w, b.reshape(b.shape[0], 1, D))
    return out.reshape(T_SMP, D)


def _ret_kernel(lg_ref, q_ref, k_ref, v_ref, gf_ref, gb_ref, *rest, seq, chunk, use_rope, has_s0, out_state):
    rest = list(rest)
    if use_rope:
        cos_ref, sin_ref = rest[0], rest[1]
        rest = rest[2:]
    if has_s0:
        s0_ref = rest[0]
        rest = rest[1:]
    y_ref = rest[0]
    rest = rest[1:]
    if out_state:
        sf_ref = rest[0]
        rest = rest[1:]
    yacc, state = rest

    h = pl.program_id(1)
    n_chunks = seq // chunk
    ri = lax.broadcasted_iota(I32, (chunk, chunk), 0)
    ci = lax.broadcasted_iota(I32, (chunk, chunk), 1)
    rel = (ri - ci).astype(F32)
    idc = lax.broadcasted_iota(I32, (chunk, 1), 0).astype(F32)
    k_scale = DK ** -0.5

    def rope(x, cs, sn):
        swapped = jnp.concatenate([pltpu.roll(x[:, :LANES], LANES // 2, 1),
                                   pltpu.roll(x[:, LANES:], LANES // 2, 1)], axis=1)
        return x * cs + swapped * sn

    for d in range(2):
        lg = lg_ref[d, h]
        if d == 0:
            mask = jnp.where(rel >= 0, jnp.exp(jnp.where(rel >= 0, rel, 0.0) * lg), 0.0)
            qdec = jnp.exp((idc + 1.0) * lg)
            kdec = jnp.exp((chunk - 1.0 - idc) * lg)
            cdec = qdec[chunk - 1:chunk, :]
        else:
            mask = jnp.where(rel <= 0, jnp.exp(jnp.where(rel <= 0, -rel, 0.0) * lg), 0.0)
            qdec = jnp.exp((chunk - idc) * lg)
            kdec = jnp.exp(idc * lg)
            cdec = qdec[0:1, :]
        g_ref = gf_ref if d == 0 else gb_ref
        if has_s0:
            state[...] = s0_ref[d]
        else:
            state[...] = jnp.zeros_like(state)

        def body(it, carry, d=d, mask=mask, qdec=qdec, kdec=kdec, cdec=cdec, g_ref=g_ref):
            n = it if d == 0 else n_chunks - 1 - it
            rows = pl.ds(pl.multiple_of(n * chunk, chunk), chunk)
            q = q_ref[rows, :].astype(F32)
            k = k_ref[rows, :].astype(F32) * k_scale
            if use_rope:
                cs = cos_ref[rows, :]
                sn = sin_ref[rows, :]
                q = rope(q, cs, sn)
                k = rope(k, cs, sn)
            v = v_ref[rows, :]
            raw = lax.dot_general(q.astype(BF16), k.astype(BF16), (((1,), (1,)), ((), ())),
                                  preferred_element_type=F32)
            s_prev = state[...]
            o = (jnp.dot((raw * mask).astype(BF16), v, preferred_element_type=F32)
                 + jnp.dot((q * qdec).astype(BF16), s_prev.astype(BF16), preferred_element_type=F32))
            state[...] = s_prev * cdec + lax.dot_general((k * kdec).astype(BF16), v, (((0,), (0,)), ((), ())),
                                                         preferred_element_type=F32)
            mu = jnp.mean(o, axis=-1, keepdims=True)
            oc = o - mu
            hn = oc * lax.rsqrt(jnp.mean(oc * oc, axis=-1, keepdims=True) + EPS)
            contrib = hn * _silu(g_ref[rows, :].astype(F32))
            if d == 0:
                yacc[rows, :] = contrib
            else:
                y_ref[rows, :] = (yacc[rows, :] + contrib).astype(y_ref.dtype)
            return carry

        lax.fori_loop(0, n_chunks, body, 0)
        if out_state:
            sf_ref[d] = state[...]


def _retention(qkv, log_g, *, ctx, rope_tabs=None, s0=None):
    seq = L_CTX if ctx else L_SMP
    nb = B_CTX if ctx else B_SMP
    rb0 = 0 if ctx else T_CTX // L_SMP
    qc, vc = QK_DIM // DK, 2 * QK_DIM // DV
    in_specs = [pl.BlockSpec((seq, DK), lambda b, h, lg: (rb0 + b, h)),
                pl.BlockSpec((seq, DK), lambda b, h, lg: (rb0 + b, qc + h)),
                pl.BlockSpec((seq, DV), lambda b, h, lg: (rb0 + b, vc + h)),
                pl.BlockSpec((seq, DV), lambda b, h, lg: (rb0 + b, vc + HEADS + h)),
                pl.BlockSpec((seq, DV), lambda b, h, lg: (rb0 + b, vc + 2 * HEADS + h))]
    args = [qkv, qkv, qkv, qkv, qkv]
    if rope_tabs is not None:
        in_specs += [pl.BlockSpec((seq, DK), lambda b, h, lg: (0, 0))] * 2
        args += list(rope_tabs)
    if s0 is not None:
        in_specs.append(pl.BlockSpec((None, 2, None, DK, DV), lambda b, h, lg: (b, 0, h, 0, 0)))
        args.append(s0)
    out_shape = [SDS((nb * seq, VDIM), BF16)]
    out_specs = [pl.BlockSpec((seq, DV), lambda b, h, lg: (b, h))]
    if ctx:
        out_shape.append(SDS((nb, 2, HEADS, DK, DV), F32))
        out_specs.append(pl.BlockSpec((None, 2, None, DK, DV), lambda b, h, lg: (b, 0, h, 0, 0)))
    return pl.pallas_call(
        functools.partial(_ret_kernel, seq=seq, chunk=RET_CHUNK, use_rope=rope_tabs is not None,
                          has_s0=s0 is not None, out_state=ctx),
        out_shape=tuple(out_shape),
        grid_spec=pltpu.PrefetchScalarGridSpec(
            num_scalar_prefetch=1, grid=(nb, HEADS), in_specs=in_specs, out_specs=tuple(out_specs),
            scratch_shapes=[pltpu.VMEM((seq, DV), F32), pltpu.VMEM((DK, DV), F32)]),
        compiler_params=_cparams(2), name="retention_ctx" if ctx else "retention_smp",
    )(log_g, *args)


def _rope_tables():
    n = DK // 4
    inv = jnp.exp(-math.log(ROPE_BASE) * jnp.arange(n, dtype=F32) / n)
    t = jnp.arange(L_SMP)
    tabs = []
    for fn, sign in ((jnp.cos, None), (jnp.sin, (-1.0, 1.0))):
        halves = []
        for pos in (t // GRID_W, t % GRID_W):
            val = fn(pos.astype(F32)[:, None] * inv[None, :])
            halves += [val, val] if sign is None else [sign[0] * val, sign[1] * val]
        tabs.append(jnp.concatenate(halves, axis=1))
    return tabs


def _top2(logits):
    lane = lax.broadcasted_iota(I32, logits.shape, 1)
    lanef = lane.astype(F32)
    lg = jnp.where(lane < N_EXPERTS, logits, -jnp.inf)
    m1 = jnp.max(lg, axis=-1, keepdims=True)
    i1 = jnp.min(jnp.where(lg == m1, lanef, float(LANES)), axis=-1, keepdims=True)
    lg2 = jnp.where(lanef == i1, -jnp.inf, lg)
    m2 = jnp.max(lg2, axis=-1, keepdims=True)
    i2 = jnp.min(jnp.where(lg2 == m2, lanef, float(LANES)), axis=-1, keepdims=True)
    e = jnp.exp(m2 - m1)
    w1 = 1.0 / (1.0 + e)
    w2 = e / (1.0 + e)
    idx = jnp.where(lane == 0, i1, jnp.where(lane == 1, i2, 0.0)).astype(I32)
    wts = jnp.where(lane == 0, w1, jnp.where(lane == 1, w2, 0.0))
    return idx, wts


def _down_kernel(*refs, n_a, n_ctx, ln_prologue, has_bias, gate_idx, sc_idx, sh_idx, router):
    refs = list(refs)
    a_refs, refs = refs[:n_a], refs[n_a:]
    w_ref, refs = refs[0], refs[1:]
    if has_bias:
        b_ref, refs = refs[0], refs[1:]
    if ln_prologue:
        (lng_ref, lnb_ref), refs = refs[:2], refs[2:]
    (x_ref, m_ref, mn_ref, gn_ref), refs = refs[:4], refs[4:]
    if router:
        wr_ref, refs = refs[0], refs[1:]
    (xo_ref, h_ref), refs = refs[:2], refs[2:]
    if router:
        (ti_ref, tw_ref), refs = refs[:2], refs[2:]
    ws = refs[0]
    i = pl.program_id(0)

    @pl.when(i == 0)
    def _():
        ws[...] = w_ref[...].astype(BF16)

    def compute(a_ref):
        a = a_ref[...]
        if ln_prologue:
            mu = jnp.mean(a, axis=-1, keepdims=True)
            ac = a - mu
            a = ac * lax.rsqrt(jnp.mean(ac * ac, axis=-1, keepdims=True) + EPS) * lng_ref[...] + lnb_ref[...]
            a = _silu(a)
        out = jnp.dot(a.astype(BF16), ws[...], preferred_element_type=F32)
        if has_bias:
            out = out + b_ref[...]
        xn = x_ref[...] + m_ref[...][gate_idx:gate_idx + 1] * out
        xo_ref[...] = xn
        mn = mn_ref[...]
        hh = _rms(xn, gn_ref[...]) * (1.0 + mn[sc_idx:sc_idx + 1]) + mn[sh_idx:sh_idx + 1]
        h_ref[...] = hh.astype(h_ref.dtype)
        if router:
            logits = jnp.dot(hh, wr_ref[...], precision=lax.Precision.HIGHEST, preferred_element_type=F32)
            ti_ref[...], tw_ref[...] = _top2(logits)

    if n_a == 1:
        compute(a_refs[0])
    else:
        pl.when(i < n_ctx)(lambda: compute(a_refs[0]))
        pl.when(i >= n_ctx)(lambda: compute(a_refs[1]))


def _down(a_list, w, layer, x, mods, *, mod_layer, bias=None, ln=None, gate_idx, norm_g, norm_layer, sc_idx, sh_idx,
          router_w=None, h_dtype=BF16, name):
    tm = 512
    n_ctx = T_CTX // tm
    kdim = w.shape[1]
    n_a = len(a_list)
    if n_a == 1:
        in_specs = [pl.BlockSpec((tm, kdim), lambda i: (i, 0))]
    else:
        in_specs = [pl.BlockSpec((tm, kdim), lambda i: (jnp.minimum(i, n_ctx - 1), 0)),
                    pl.BlockSpec((tm, kdim), lambda i: (jnp.maximum(i - n_ctx, 0), 0))]
    args = list(a_list)
    in_specs.append(pl.BlockSpec((None, kdim, D), lambda i: (layer, 0, 0), pipeline_mode=pl.Buffered(1)))
    args.append(w)
    row = lambda i: (0, 0)
    if bias is not None:
        in_specs.append(pl.BlockSpec((None, 1, D), lambda i: (layer, 0, 0)))
        args.append(bias.reshape(bias.shape[0], 1, D))
    if ln is not None:
        in_specs += [pl.BlockSpec((None, 1, D), lambda i: (layer, 0, 0))] * 2
        args += [ln[0].reshape(-1, 1, D), ln[1].reshape(-1, 1, D)]
    in_specs += [pl.BlockSpec((tm, D), lambda i: (i, 0)),
                 pl.BlockSpec((None, None, 6, D), lambda i: (mod_layer, _cond_row(i, tm), 0, 0)),
                 pl.BlockSpec((None, None, 6, D), lambda i: (norm_layer, _cond_row(i, tm), 0, 0)),
                 pl.BlockSpec((1, D), row)]
    args += [x, mods, mods, norm_g]
    out_shape = [SDS((T, D), F32), SDS((T, D), h_dtype)]
    out_specs = [pl.BlockSpec((tm, D), lambda i: (i, 0)), pl.BlockSpec((tm, D), lambda i: (i, 0))]
    if router_w is not None:
        in_specs.append(pl.BlockSpec((D, LANES), row))
        args.append(router_w)
        out_shape += [SDS((T, LANES), I32), SDS((T, LANES), F32)]
        out_specs += [pl.BlockSpec((tm, LANES), lambda i: (i, 0))] * 2
    return pl.pallas_call(
        functools.partial(_down_kernel, n_a=n_a, n_ctx=n_ctx, ln_prologue=ln is not None, has_bias=bias is not None,
                          gate_idx=gate_idx, sc_idx=sc_idx, sh_idx=sh_idx, router=router_w is not None),
        out_shape=tuple(out_shape), grid=(T // tm,), in_specs=in_specs, out_specs=tuple(out_specs),
        scratch_shapes=[pltpu.VMEM((kdim, D), BF16)],
        compiler_params=_cparams(1), name=name,
    )(*args)


DISPATCH_CHUNK = 128


def _dispatch_kernel(pos_ref, lt_ref, ne_ref, h_ref, z_ref, xs_ref, sem, zsem):
    for e in range(N_EXPERTS):
        @pl.when(ne_ref[e] > 0)
        def _(e=e):
            start = pl.multiple_of(lt_ref[e] * TM_MOE, TM_MOE)
            pltpu.make_async_copy(z_ref, xs_ref.at[pl.ds(start, TM_MOE)], zsem).start()
    for e in range(N_EXPERTS):
        @pl.when(ne_ref[e] > 0)
        def _():
            pltpu.make_async_copy(z_ref, xs_ref.at[pl.ds(0, TM_MOE)], zsem).wait()

    def row_copy(t, p):
        return pltpu.make_async_copy(h_ref.at[pl.ds(t, 1)], xs_ref.at[pl.ds(p, 1)], sem)

    def drain():
        def wait_one(r, c):
            row_copy(0, 0).wait()
            return c
        lax.fori_loop(0, 2 * DISPATCH_CHUNK, wait_one, 0)

    def chunk_body(c, carry):
        def issue(r, cc):
            t = c * DISPATCH_CHUNK + r
            row_copy(t, pos_ref[2 * t]).start()
            row_copy(t, pos_ref[2 * t + 1]).start()
            return cc
        lax.fori_loop(0, DISPATCH_CHUNK, issue, 0)
        pl.when(c > 0)(drain)
        return carry

    lax.fori_loop(0, T // DISPATCH_CHUNK, chunk_body, 0)
    drain()


def _dispatch(pos, last_tile, n_tiles_e, h, zeros_tile):
    return pl.pallas_call(
        _dispatch_kernel, out_shape=SDS((N_SLOTS, D), F32),
        grid_spec=pltpu.PrefetchScalarGridSpec(
            num_scalar_prefetch=3, grid=(1,),
            in_specs=[pl.BlockSpec(memory_space=pl.ANY), pl.BlockSpec(memory_space=pl.ANY)],
            out_specs=pl.BlockSpec(memory_space=pl.ANY),
            scratch_shapes=[pltpu.SemaphoreType.DMA, pltpu.SemaphoreType.DMA]),
        compiler_params=_cparams(1), name="moe_dispatch",
    )(pos, last_tile, n_tiles_e, h, zeros_tile)


def _expert_changed(te_ref, t):
    return jnp.logical_or(t == 0, te_ref[t] != te_ref[jnp.maximum(t - 1, 0)])


def _moe_gu_kernel(te_ref, nt_ref, x_ref, wg_ref, wu_ref, o_ref, wgs, wus):
    t = pl.program_id(1)

    @pl.when(_expert_changed(te_ref, t))
    def _():
        wgs[...] = wg_ref[...].astype(BF16)
        wus[...] = wu_ref[...].astype(BF16)

    @pl.when(t < nt_ref[0])
    def _():
        x = x_ref[...].astype(BF16)
        g = jnp.dot(x, wgs[...], preferred_element_type=F32)
        u = jnp.dot(x, wus[...], preferred_element_type=F32)
        o_ref[...] = (_silu(g) * u).astype(o_ref.dtype)

    @pl.when(t >= nt_ref[0])
    def _():
        o_ref[...] = jnp.zeros_like(o_ref)


def _moe_gu(tile_expert, n_tiles, xs, w_gu, layer):
    tn = 896
    nj = E_FF // tn
    xmap = lambda j, t, te, nt: (jnp.minimum(t, nt[0] - 1), 0)
    return pl.pallas_call(
        _moe_gu_kernel, out_shape=SDS((N_SLOTS, E_FF), BF16),
        grid_spec=pltpu.PrefetchScalarGridSpec(
            num_scalar_prefetch=2, grid=(nj, NT_MOE),
            in_specs=[pl.BlockSpec((TM_MOE, D), xmap),
                      pl.BlockSpec((None, None, D, tn), lambda j, t, te, nt: (layer, te[t], 0, j)),
                      pl.BlockSpec((None, None, D, tn), lambda j, t, te, nt: (layer, te[t], 0, j + nj))],
            out_specs=pl.BlockSpec((TM_MOE, tn), lambda j, t, te, nt: (t, j)),
            scratch_shapes=[pltpu.VMEM((D, tn), BF16), pltpu.VMEM((D, tn), BF16)]),
        compiler_params=_cparams(2), name="moe_gate_up",
    )(tile_expert, n_tiles, xs, w_gu, w_gu)


def _moe_down_kernel(te_ref, nt_ref, h_ref, wd_ref, o_ref, wds):
    t = pl.program_id(1)

    @pl.when(_expert_changed(te_ref, t))
    def _():
        wds[...] = wd_ref[...].astype(BF16)

    @pl.when(t < nt_ref[0])
    def _():
        o_ref[...] = jnp.dot(h_ref[...], wds[...], preferred_element_type=F32)

    @pl.when(t >= nt_ref[0])
    def _():
        o_ref[...] = jnp.zeros_like(o_ref)


def _moe_down(tile_expert, n_tiles, hid, w_down, layer):
    tn = 512
    hmap = lambda j, t, te, nt: (jnp.minimum(t, nt[0] - 1), 0)
    return pl.pallas_call(
        _moe_down_kernel, out_shape=SDS((N_SLOTS, D), F32),
        grid_spec=pltpu.PrefetchScalarGridSpec(
            num_scalar_prefetch=2, grid=(D // tn, NT_MOE),
            in_specs=[pl.BlockSpec((TM_MOE, E_FF), hmap),
                      pl.BlockSpec((None, None, E_FF, tn), lambda j, t, te, nt: (layer, te[t], 0, j))],
            out_specs=pl.BlockSpec((TM_MOE, tn), lambda j, t, te, nt: (t, j)),
            scratch_shapes=[pltpu.VMEM((E_FF, tn), BF16)]),
        compiler_params=_cparams(2), name="moe_down",
    )(tile_expert, n_tiles, hid, w_down)


TM_COMBINE = 256


def _combine_kernel(pos_ref, y_ref, tw_ref, x_ref, m_ref, mn_ref, gn_ref, *rest, final, n_ctx):
    if final:
        yc_ref, ys_ref, buf0, buf1, sem = rest
    else:
        xo_ref, h_ref, buf0, buf1, sem = rest
    i = pl.program_id(0)

    def row_copy(p, buf, r):
        return pltpu.make_async_copy(y_ref.at[pl.ds(p, 1)], buf.at[pl.ds(r, 1)], sem)

    def issue(r, c):
        t = i * TM_COMBINE + r
        row_copy(pos_ref[2 * t], buf0, r).start()
        row_copy(pos_ref[2 * t + 1], buf1, r).start()
        return c

    lax.fori_loop(0, TM_COMBINE, issue, 0)

    def wait_one(r, c):
        row_copy(0, buf0, 0).wait()
        return c

    lax.fori_loop(0, 2 * TM_COMBINE, wait_one, 0)

    tw = tw_ref[...]
    f = tw[:, 0:1] * buf0[...] + tw[:, 1:2] * buf1[...]
    xn = x_ref[...] + m_ref[...][5:6] * f
    y = _rms(xn, gn_ref[...])
    if final:
        @pl.when(i < n_ctx)
        def _():
            yc_ref[...] = y

        @pl.when(i >= n_ctx)
        def _():
            ys_ref[...] = y
    else:
        mn = mn_ref[...]
        xo_ref[...] = xn
        h_ref[...] = (y * (1.0 + mn[1:2]) + mn[0:1]).astype(h_ref.dtype)


def _combine(pos, y, tw, x, mods, layer, norm_g, *, final):
    tm = TM_COMBINE
    n_ctx = T_CTX // tm
    norm_layer = layer if final else layer + 1
    tile = lambda i, p: (i, 0)
    in_specs = [pl.BlockSpec(memory_space=pl.ANY),
                pl.BlockSpec((tm, LANES), tile),
                pl.BlockSpec((tm, D), tile),
                pl.BlockSpec((None, None, 6, D), lambda i, p: (layer, _cond_row(i, tm), 0, 0)),
                pl.BlockSpec((None, None, 6, D), lambda i, p: (norm_layer, _cond_row(i, tm), 0, 0)),
                pl.BlockSpec((1, D), lambda i, p: (0, 0))]
    if final:
        out_shape = (SDS((T_CTX, D), F32), SDS((T_SMP, D), F32))
        out_specs = (pl.BlockSpec((tm, D), lambda i, p: (jnp.minimum(i, n_ctx - 1), 0)),
                     pl.BlockSpec((tm, D), lambda i, p: (jnp.maximum(i - n_ctx, 0), 0)))
    else:
        out_shape = (SDS((T, D), F32), SDS((T, D), BF16))
        out_specs = (pl.BlockSpec((tm, D), tile), pl.BlockSpec((tm, D), tile))
    return pl.pallas_call(
        functools.partial(_combine_kernel, final=final, n_ctx=n_ctx), out_shape=out_shape,
        grid_spec=pltpu.PrefetchScalarGridSpec(
            num_scalar_prefetch=1, grid=(T // tm,), in_specs=in_specs, out_specs=out_specs,
            scratch_shapes=[pltpu.VMEM((tm, D), F32), pltpu.VMEM((tm, D), F32), pltpu.SemaphoreType.DMA]),
        compiler_params=_cparams(1), name="moe_combine",
    )(pos, y, tw, x, mods, mods, norm_g)


def _route(top_i):
    i1, i2 = top_i[:, 0], top_i[:, 1]
    oh1 = jax.nn.one_hot(i1, N_EXPERTS, dtype=I32)
    oh2 = jax.nn.one_hot(i2, N_EXPERTS, dtype=I32)
    sel = oh1 + oh2
    csum = jnp.cumsum(sel, axis=0)
    rank = csum - sel
    counts = csum[-1]
    n_tiles_e = (counts + TM_MOE - 1) // TM_MOE
    tile_end = jnp.cumsum(n_tiles_e)
    tile_start = tile_end - n_tiles_e
    row_start = tile_start * TM_MOE
    pos1 = jnp.sum((row_start[None, :] + rank) * oh1, axis=1)
    pos2 = jnp.sum((row_start[None, :] + rank) * oh2, axis=1)
    pos = jnp.stack([pos1, pos2], axis=1).reshape(-1).astype(I32)
    total = tile_end[-1]
    tids = jnp.minimum(jnp.arange(NT_MOE, dtype=I32), total - 1)
    tile_expert = jnp.sum(tids[:, None] >= tile_end[None, :], axis=1).astype(I32)
    last_tile = (tile_end - 1).astype(I32)
    return pos, tile_expert, total.reshape(1).astype(I32), last_tile, n_tiles_e.astype(I32)


def _moe(h, top_i, top_w, x, mods, layer, j, w_gu, w_down, norm_g, *, final):
    pos, tile_expert, n_tiles, last_tile, n_tiles_e = _route(top_i)
    xs = _dispatch(pos, last_tile, n_tiles_e, h, jnp.zeros((TM_MOE, D), F32))
    hid = _moe_gu(tile_expert, n_tiles, xs, w_gu, j)
    y = _moe_down(tile_expert, n_tiles, hid, w_down, j)
    return _combine(pos, y, top_w, x, mods, layer, norm_g, final=final)


def kernel(x_prompt, x_sample, state_ret, c, c_ctx, w_ada, b_ada, norm_mix, norm_ffn, norm_final,
           conv_w_pw1, conv_b_pw1, conv_w_dw, conv_b_dw, conv_ln_g, conv_ln_b, conv_w_pw2, conv_b_pw2,
           ret_w_in, ret_decay_logit, ret_w_out, ffn_w_gu, ffn_w_down, moe_w_router, moe_w_gu, moe_w_down):
    cond = jnp.zeros((N_COND, D), F32).at[0].set(c_ctx).at[1:1 + B_SMP].set(c)
    mods = _ada(cond, w_ada, b_ada).reshape(DEPTH, N_COND, 6, D)
    log_g = jax.nn.log_sigmoid(ret_decay_logit.astype(F32))
    rope_tabs = _rope_tables()
    router_w = jnp.pad(moe_w_router, ((0, 0), (0, 0), (0, LANES - N_EXPERTS)))

    x, h = _norm0(x_prompt.reshape(T_CTX, D), x_sample.reshape(T_SMP, D), norm_mix[0:1], mods)
    states = []
    out = None
    for i in range(DEPTH):
        j = i // 2
        if i % 2 == 0:
            u = _mm_pair(h, conv_w_pw1, j, conv_b_pw1, D, D, "glu", F32)
            cv_c = _conv_ctx(u, conv_w_dw, conv_b_dw, j)
            cv_s = _conv_smp(u, conv_w_dw, conv_b_dw, j)
            x, h = _down([cv_c, cv_s], conv_w_pw2, j, x, mods, mod_layer=i, bias=conv_b_pw2, ln=(conv_ln_g, conv_ln_b),
                         gate_idx=2, norm_g=norm_ffn[i:i + 1], norm_layer=i, sc_idx=4, sh_idx=3, name="down_conv")
            hid = _mm_pair(h, ffn_w_gu, j, None, D_FF, D_FF // 2, "swiglu", BF16)
            x, h = _down([hid], ffn_w_down, j, x, mods, mod_layer=i, gate_idx=5, norm_g=norm_mix[i + 1:i + 2],
                         norm_layer=i + 1, sc_idx=1, sh_idx=0, name="down_ffn")
        else:
            qkv = _mm(h, ret_w_in, j, RET_IN, BF16)
            y_c, s_fin = _retention(qkv, log_g[j], ctx=True)
            (y_s,) = _retention(qkv, log_g[j], ctx=False, rope_tabs=rope_tabs, s0=state_ret[:, j])
            states.append(s_fin)
            x, h, top_i, top_w = _down([y_c, y_s], ret_w_out, j, x, mods, mod_layer=i, gate_idx=2, norm_g=norm_ffn[i:i + 1],
                                       norm_layer=i, sc_idx=4, sh_idx=3, router_w=router_w[j], h_dtype=F32,
                                       name="down_ret")
            final = i == DEPTH - 1
            res = _moe(h, top_i, top_w, x, mods, i, j, moe_w_gu, moe_w_down,
                       norm_final.reshape(1, D) if final else norm_mix[i + 1:i + 2], final=final)
            if final:
                out = res
            else:
                x, h = res
    y_prompt = out[0].reshape(B_CTX, L_CTX, D)
    y_sample = out[1].reshape(B_SMP, L_SMP, D)
    state_new = jnp.stack(states, axis=1).astype(x_prompt.dtype)
    return (y_prompt, y_sample, state_new)
```

```python
import functools
import math

import jax
import jax.numpy as jnp
from jax import lax
from jax.experimental import pallas as pl
from jax.experimental.pallas import tpu as pltpu

F32 = jnp.float32
BF16 = jnp.bfloat16
I32 = jnp.int32
SDS = jax.ShapeDtypeStruct

D = 1024
B_CTX, L_CTX = 32, 256
B_SMP, L_SMP = 2, 2048
T_CTX = B_CTX * L_CTX
T_SMP = B_SMP * L_SMP
T = T_CTX + T_SMP
GRID_W = 64
GRID_H = L_SMP // GRID_W
DEPTH = 4
CONV_WIDTH = 31
CONV_PAD = CONV_WIDTH // 2
HEADS = 4
DK = 256
DV = 512
QK_DIM = HEADS * DK
VDIM = HEADS * DV
RET_IN = 2 * QK_DIM + 3 * VDIM
RET_CHUNK = 128
ROPE_BASE = 10000.0
D_FF = 2816
N_EXPERTS = 8
E_FF = 3584
EPS = 1e-6
LANES = 128
N_COND = 8
VMEM_LIMIT = 56 * 1024 * 1024

TM_MOE = 512
NT_MOE = (2 * T) // TM_MOE + N_EXPERTS
N_SLOTS = NT_MOE * TM_MOE


def _cparams(n_axes):
    return pltpu.CompilerParams(dimension_semantics=("arbitrary",) * n_axes,
                                vmem_limit_bytes=VMEM_LIMIT)


def _cond_row(i, tm):
    return jnp.maximum(0, (i * tm - T_CTX) // L_SMP + 1)


def _silu(x):
    return x * jax.nn.sigmoid(x)


def _rms(x, g):
    return x * lax.rsqrt(jnp.mean(x * x, axis=-1, keepdims=True) + EPS) * g


def _ada_kernel(c_ref, w_ref, b_ref, o_ref):
    s = _silu(c_ref[...]).astype(BF16)
    o_ref[...] = jnp.dot(s, w_ref[...].astype(BF16), preferred_element_type=F32) + b_ref[...]


def _ada(cond, w_ada, b_ada):
    tn = 1536
    n = 6 * D
    return pl.pallas_call(
        _ada_kernel, out_shape=SDS((DEPTH, N_COND, n), F32), grid=(DEPTH, n // tn),
        in_specs=[pl.BlockSpec((N_COND, D), lambda l, j: (0, 0)),
                  pl.BlockSpec((None, D, tn), lambda l, j: (l, 0, j)),
                  pl.BlockSpec((None, 1, tn), lambda l, j: (l, 0, j))],
        out_specs=pl.BlockSpec((None, N_COND, tn), lambda l, j: (l, 0, j)),
        compiler_params=_cparams(2), name="ada_mod",
    )(cond, w_ada, b_ada.reshape(DEPTH, 1, n))


def _norm0_kernel(xc_ref, xs_ref, g_ref, m_ref, xo_ref, h_ref, *, n_ctx):
    i = pl.program_id(0)

    def emit(x_ref):
        x = x_ref[...]
        xo_ref[...] = x
        m = m_ref[...]
        h_ref[...] = (_rms(x, g_ref[...]) * (1.0 + m[1:2]) + m[0:1]).astype(h_ref.dtype)

    pl.when(i < n_ctx)(lambda: emit(xc_ref))
    pl.when(i >= n_ctx)(lambda: emit(xs_ref))


def _norm0(xc, xs, g, mods):
    tm = 512
    n_ctx = T_CTX // tm
    return pl.pallas_call(
        functools.partial(_norm0_kernel, n_ctx=n_ctx),
        out_shape=(SDS((T, D), F32), SDS((T, D), BF16)), grid=(T // tm,),
        in_specs=[pl.BlockSpec((tm, D), lambda i: (jnp.minimum(i, n_ctx - 1), 0)),
                  pl.BlockSpec((tm, D), lambda i: (jnp.maximum(i - n_ctx, 0), 0)),
                  pl.BlockSpec((1, D), lambda i: (0, 0)),
                  pl.BlockSpec((None, None, 6, D), lambda i: (0, _cond_row(i, tm), 0, 0))],
        out_specs=(pl.BlockSpec((tm, D), lambda i: (i, 0)), pl.BlockSpec((tm, D), lambda i: (i, 0))),
        compiler_params=_cparams(1), name="norm0",
    )(xc, xs, g, mods)


def _mm_pair_kernel(*refs, act, has_bias):
    if has_bias:
        a_ref, w1_ref, w2_ref, b1_ref, b2_ref, o_ref, w1s, w2s = refs
    else:
        a_ref, w1_ref, w2_ref, o_ref, w1s, w2s = refs

    @pl.when(pl.program_id(1) == 0)
    def _():
        w1s[...] = w1_ref[...].astype(BF16)
        w2s[...] = w2_ref[...].astype(BF16)

    a = a_ref[...]
    p = jnp.dot(a, w1s[...], preferred_element_type=F32)
    q = jnp.dot(a, w2s[...], preferred_element_type=F32)
    if has_bias:
        p = p + b1_ref[...]
        q = q + b2_ref[...]
    r = p * jax.nn.sigmoid(q) if act == "glu" else _silu(p) * q
    o_ref[...] = r.astype(o_ref.dtype)


def _mm_pair(a, w, layer, bias, n_half, tn, act, out_dtype):
    tm = 1024
    nj = n_half // tn
    in_specs = [pl.BlockSpec((tm, D), lambda j, i: (i, 0)),
                pl.BlockSpec((None, D, tn), lambda j, i: (layer, 0, j)),
                pl.BlockSpec((None, D, tn), lambda j, i: (layer, 0, j + nj))]
    args = [a, w, w]
    if bias is not None:
        b3 = bias.reshape(bias.shape[0], 1, 2 * n_half)
        in_specs += [pl.BlockSpec((None, 1, tn), lambda j, i: (layer, 0, j)),
                     pl.BlockSpec((None, 1, tn), lambda j, i: (layer, 0, j + nj))]
        args += [b3, b3]
    return pl.pallas_call(
        functools.partial(_mm_pair_kernel, act=act, has_bias=bias is not None),
        out_shape=SDS((T, n_half), out_dtype), grid=(nj, T // tm),
        in_specs=in_specs, out_specs=pl.BlockSpec((tm, tn), lambda j, i: (i, j)),
        scratch_shapes=[pltpu.VMEM((D, tn), BF16), pltpu.VMEM((D, tn), BF16)],
        compiler_params=_cparams(2), name="mm_" + act,
    )(*args)


def _mm_kernel(a_ref, w_ref, o_ref, ws):
    @pl.when(pl.program_id(1) == 0)
    def _():
        ws[...] = w_ref[...].astype(BF16)

    o_ref[...] = jnp.dot(a_ref[...], ws[...], preferred_element_type=F32).astype(o_ref.dtype)


def _mm(a, w, layer, n, out_dtype):
    tm, tn = 1024, 1024
    return pl.pallas_call(
        _mm_kernel, out_shape=SDS((T, n), out_dtype), grid=(n // tn, T // tm),
        in_specs=[pl.BlockSpec((tm, D), lambda j, i: (i, 0)),
                  pl.BlockSpec((None, D, tn), lambda j, i: (layer, 0, j))],
        out_specs=pl.BlockSpec((tm, tn), lambda j, i: (i, j)),
        scratch_shapes=[pltpu.VMEM((D, tn), BF16)],
        compiler_params=_cparams(2), name="mm_ret_in",
    )(a, w)


CONV_LB = 512
CONV_RC = 64
CONV_OFF = 16


def _conv_ctx_kernel(u_ref, w_ref, b_ref, o_ref, pad):
    zeros = jnp.zeros((CONV_OFF, CONV_LB), F32)
    pad[0:CONV_OFF, :] = zeros
    pad[CONV_OFF + L_CTX:CONV_OFF + L_CTX + CONV_OFF, :] = zeros
    pad[CONV_OFF:CONV_OFF + L_CTX, :] = u_ref[...]
    for lb in range(CONV_LB // LANES):
        ls = slice(lb * LANES, (lb + 1) * LANES)
        for rc in range(L_CTX // CONV_RC):
            acc = jnp.zeros((CONV_RC, LANES), F32)
            for k in range(CONV_WIDTH):
                start = rc * CONV_RC + CONV_OFF - CONV_PAD + k
                acc = acc + w_ref[k:k + 1, ls] * pad[start:start + CONV_RC, ls]
            o_ref[rc * CONV_RC:(rc + 1) * CONV_RC, ls] = acc + b_ref[:, ls]


def _conv_ctx(u, w, b, layer):
    return pl.pallas_call(
        _conv_ctx_kernel, out_shape=SDS((T_CTX, D), F32), grid=(B_CTX, D // CONV_LB),
        in_specs=[pl.BlockSpec((L_CTX, CONV_LB), lambda b, j: (b, j)),
                  pl.BlockSpec((None, CONV_WIDTH, CONV_LB), lambda b, j: (layer, 0, j)),
                  pl.BlockSpec((None, 1, CONV_LB), lambda b, j: (layer, 0, j))],
        out_specs=pl.BlockSpec((L_CTX, CONV_LB), lambda b, j: (b, j)),
        scratch_shapes=[pltpu.VMEM((L_CTX + 2 * CONV_OFF, CONV_LB), F32)],
        compiler_params=_cparams(2), name="conv_ctx",
    )(u, w, b.reshape(b.shape[0], 1, D))


def _conv_smp_kernel(u_ref, w_ref, b_ref, o_ref, pad_r, pad_c):
    j = pl.program_id(1)

    @pl.when(j < (D // 2) // LANES)
    def _():
        zeros = jnp.zeros((GRID_H, CONV_OFF, LANES), F32)
        pad_r[:, 0:CONV_OFF, :] = zeros
        pad_r[:, CONV_OFF + GRID_W:, :] = zeros
        pad_r[:, CONV_OFF:CONV_OFF + GRID_W, :] = u_ref[...]

        def body(r, c):
            acc = jnp.zeros((GRID_W, LANES), F32)
            for k in range(CONV_WIDTH):
                acc = acc + w_ref[k:k + 1, :] * pad_r[r, pl.ds(CONV_OFF - CONV_PAD + k, GRID_W), :]
            o_ref[r] = acc + b_ref[...]
            return c

        lax.fori_loop(0, GRID_H, body, 0)

    @pl.when(j >= (D // 2) // LANES)
    def _():
        zeros = jnp.zeros((CONV_PAD, GRID_W, LANES), F32)
        pad_c[0:CONV_PAD] = zeros
        pad_c[CONV_PAD + GRID_H:] = zeros
        pad_c[CONV_PAD:CONV_PAD + GRID_H] = u_ref[...]

        def body(r, c):
            acc = jnp.zeros((GRID_W, LANES), F32)
            for k in range(CONV_WIDTH):
                acc = acc + w_ref[k:k + 1, :] * pad_c[r + k]
            o_ref[r] = acc + b_ref[...]
            return c

        lax.fori_loop(0, GRID_H, body, 0)


def _conv_smp(u, w, b, layer):
    u3 = u.reshape(T // GRID_W, GRID_W, D)
    first = T_CTX // GRID_W // GRID_H
    out = pl.pallas_call(
        _conv_smp_kernel, out_shape=SDS((T_SMP // GRID_W, GRID_W, D), F32), grid=(B_SMP, D // LANES),
        in_specs=[pl.BlockSpec((GRID_H, GRID_W, LANES), lambda b, j: (first + b, 0, j)),
                  pl.BlockSpec((None, CONV_WIDTH, LANES), lambda b, j: (layer, 0, j)),
                  pl.BlockSpec((None, 1, LANES), lambda b, j: (layer, 0, j))],
        out_specs=pl.BlockSpec((GRID_H, GRID_W, LANES), lambda b, j: (b, 0, j)),
        scratch_shapes=[pltpu.VMEM((GRID_H, GRID_W + 2 * CONV_OFF, LANES), F32),
                        pltpu.VMEM((GRID_H + 2 * CONV_PAD, GRID_W, LANES), F32)],
        compiler_params=_cparams(2), name="conv_smp",
    )(u3, w, b.reshape(b.shape[0], 1, D))
    return out.reshape(T_SMP, D)


def _ret_kernel(lg_ref, q_ref, k_ref, v_ref, gf_ref, gb_ref, *rest, seq, chunk, use_rope, has_s0, out_state):
    rest = list(rest)
    if use_rope:
        cos_ref, sin_ref = rest[0], rest[1]
        rest = rest[2:]
    if has_s0:
        s0_ref = rest[0]
        rest = rest[1:]
    y_ref = rest[0]
    rest = rest[1:]
    if out_state:
        sf_ref = rest[0]
        rest = rest[1:]
    yacc, state = rest

    h = pl.program_id(1)
    n_chunks = seq // chunk
    ri = lax.broadcasted_iota(I32, (chunk, chunk), 0)
    ci = lax.broadcasted_iota(I32, (chunk, chunk), 1)
    rel = (ri - ci).astype(F32)
    idc = lax.broadcasted_iota(I32, (chunk, 1), 0).astype(F32)
    k_scale = DK ** -0.5

    def rope(x, cs, sn):
        swapped = jnp.concatenate([pltpu.roll(x[:, :LANES], LANES // 2, 1),
                                   pltpu.roll(x[:, LANES:], LANES // 2, 1)], axis=1)
        return x * cs + swapped * sn

    for d in range(2):
        lg = lg_ref[d, h]
        if d == 0:
            mask = jnp.where(rel >= 0, jnp.exp(jnp.where(rel >= 0, rel, 0.0) * lg), 0.0)
            qdec = jnp.exp((idc + 1.0) * lg)
            kdec = jnp.exp((chunk - 1.0 - idc) * lg)
            cdec = qdec[chunk - 1:chunk, :]
        else:
            mask = jnp.where(rel <= 0, jnp.exp(jnp.where(rel <= 0, -rel, 0.0) * lg), 0.0)
            qdec = jnp.exp((chunk - idc) * lg)
            kdec = jnp.exp(idc * lg)
            cdec = qdec[0:1, :]
        g_ref = gf_ref if d == 0 else gb_ref
        if has_s0:
            state[...] = s0_ref[d]
        else:
            state[...] = jnp.zeros_like(state)

        def body(it, carry, d=d, mask=mask, qdec=qdec, kdec=kdec, cdec=cdec, g_ref=g_ref):
            n = it if d == 0 else n_chunks - 1 - it
            rows = pl.ds(pl.multiple_of(n * chunk, chunk), chunk)
            q = q_ref[rows, :].astype(F32)
            k = k_ref[rows, :].astype(F32) * k_scale
            if use_rope:
                cs = cos_ref[rows, :]
                sn = sin_ref[rows, :]
                q = rope(q, cs, sn)
                k = rope(k, cs, sn)
            v = v_ref[rows, :]
            raw = lax.dot_general(q.astype(BF16), k.astype(BF16), (((1,), (1,)), ((), ())),
                                  preferred_element_type=F32)
            s_prev = state[...]
            o = (jnp.dot((raw * mask).astype(BF16), v, preferred_element_type=F32)
                 + jnp.dot((q * qdec).astype(BF16), s_prev.astype(BF16), preferred_element_type=F32))
            state[...] = s_prev * cdec + lax.dot_general((k * kdec).astype(BF16), v, (((0,), (0,)), ((), ())),
                                                         preferred_element_type=F32)
            mu = jnp.mean(o, axis=-1, keepdims=True)
            oc = o - mu
            hn = oc * lax.rsqrt(jnp.mean(oc * oc, axis=-1, keepdims=True) + EPS)
            contrib = hn * _silu(g_ref[rows, :].astype(F32))
            if d == 0:
                yacc[rows, :] = contrib
            else:
                y_ref[rows, :] = (yacc[rows, :] + contrib).astype(y_ref.dtype)
            return carry

        lax.fori_loop(0, n_chunks, body, 0)
        if out_state:
            sf_ref[d] = state[...]


def _retention(qkv, log_g, *, ctx, rope_tabs=None, s0=None):
    seq = L_CTX if ctx else L_SMP
    nb = B_CTX if ctx else B_SMP
    rb0 = 0 if ctx else T_CTX // L_SMP
    qc, vc = QK_DIM // DK, 2 * QK_DIM // DV
    in_specs = [pl.BlockSpec((seq, DK), lambda b, h, lg: (rb0 + b, h)),
                pl.BlockSpec((seq, DK), lambda b, h, lg: (rb0 + b, qc + h)),
                pl.BlockSpec((seq, DV), lambda b, h, lg: (rb0 + b, vc + h)),
                pl.BlockSpec((seq, DV), lambda b, h, lg: (rb0 + b, vc + HEADS + h)),
                pl.BlockSpec((seq, DV), lambda b, h, lg: (rb0 + b, vc + 2 * HEADS + h))]
    args = [qkv, qkv, qkv, qkv, qkv]
    if rope_tabs is not None:
        in_specs += [pl.BlockSpec((seq, DK), lambda b, h, lg: (0, 0))] * 2
        args += list(rope_tabs)
    if s0 is not None:
        in_specs.append(pl.BlockSpec((None, 2, None, DK, DV), lambda b, h, lg: (b, 0, h, 0, 0)))
        args.append(s0)
    out_shape = [SDS((nb * seq, VDIM), BF16)]
    out_specs = [pl.BlockSpec((seq, DV), lambda b, h, lg: (b, h))]
    if ctx:
        out_shape.append(SDS((nb, 2, HEADS, DK, DV), F32))
        out_specs.append(pl.BlockSpec((None, 2, None, DK, DV), lambda b, h, lg: (b, 0, h, 0, 0)))
    return pl.pallas_call(
        functools.partial(_ret_kernel, seq=seq, chunk=RET_CHUNK, use_rope=rope_tabs is not None,
                          has_s0=s0 is not None, out_state=ctx),
        out_shape=tuple(out_shape),
        grid_spec=pltpu.PrefetchScalarGridSpec(
            num_scalar_prefetch=1, grid=(nb, HEADS), in_specs=in_specs, out_specs=tuple(out_specs),
            scratch_shapes=[pltpu.VMEM((seq, DV), F32), pltpu.VMEM((DK, DV), F32)]),
        compiler_params=_cparams(2), name="retention_ctx" if ctx else "retention_smp",
    )(log_g, *args)


def _rope_tables():
    n = DK // 4
    inv = jnp.exp(-math.log(ROPE_BASE) * jnp.arange(n, dtype=F32) / n)
    t = jnp.arange(L_SMP)
    tabs = []
    for fn, sign in ((jnp.cos, None), (jnp.sin, (-1.0, 1.0))):
        halves = []
        for pos in (t // GRID_W, t % GRID_W):
            val = fn(pos.astype(F32)[:, None] * inv[None, :])
            halves += [val, val] if sign is None else [sign[0] * val, sign[1] * val]
        tabs.append(jnp.concatenate(halves, axis=1))
    return tabs


def _top2(logits):
    lane = lax.broadcasted_iota(I32, logits.shape, 1)
    lanef = lane.astype(F32)
    lg = jnp.where(lane < N_EXPERTS, logits, -jnp.inf)
    m1 = jnp.max(lg, axis=-1, keepdims=True)
    i1 = jnp.min(jnp.where(lg == m1, lanef, float(LANES)), axis=-1, keepdims=True)
    lg2 = jnp.where(lanef == i1, -jnp.inf, lg)
    m2 = jnp.max(lg2, axis=-1, keepdims=True)
    i2 = jnp.min(jnp.where(lg2 == m2, lanef, float(LANES)), axis=-1, keepdims=True)
    e = jnp.exp(m2 - m1)
    w1 = 1.0 / (1.0 + e)
    w2 = e / (1.0 + e)
    idx = jnp.where(lane == 0, i1, jnp.where(lane == 1, i2, 0.0)).astype(I32)
    wts = jnp.where(lane == 0, w1, jnp.where(lane == 1, w2, 0.0))
    return idx, wts


def _down_kernel(*refs, n_a, n_ctx, ln_prologue, has_bias, gate_idx, sc_idx, sh_idx, router):
    refs = list(refs)
    a_refs, refs = refs[:n_a], refs[n_a:]
    w_ref, refs = refs[0], refs[1:]
    if has_bias:
        b_ref, refs = refs[0], refs[1:]
    if ln_prologue:
        (lng_ref, lnb_ref), refs = refs[:2], refs[2:]
    (x_ref, m_ref, mn_ref, gn_ref), refs = refs[:4], refs[4:]
    if router:
        wr_ref, refs = refs[0], refs[1:]
    (xo_ref, h_ref), refs = refs[:2], refs[2:]
    if router:
        (ti_ref, tw_ref), refs = refs[:2], refs[2:]
    ws = refs[0]
    i = pl.program_id(0)

    @pl.when(i == 0)
    def _():
        ws[...] = w_ref[...].astype(BF16)

    def compute(a_ref):
        a = a_ref[...]
        if ln_prologue:
            mu = jnp.mean(a, axis=-1, keepdims=True)
            ac = a - mu
            a = ac * lax.rsqrt(jnp.mean(ac * ac, axis=-1, keepdims=True) + EPS) * lng_ref[...] + lnb_ref[...]
            a = _silu(a)
        out = jnp.dot(a.astype(BF16), ws[...], preferred_element_type=F32)
        if has_bias:
            out = out + b_ref[...]
        xn = x_ref[...] + m_ref[...][gate_idx:gate_idx + 1] * out
        xo_ref[...] = xn
        mn = mn_ref[...]
        hh = _rms(xn, gn_ref[...]) * (1.0 + mn[sc_idx:sc_idx + 1]) + mn[sh_idx:sh_idx + 1]
        h_ref[...] = hh.astype(h_ref.dtype)
        if router:
            logits = jnp.dot(hh, wr_ref[...], precision=lax.Precision.HIGHEST, preferred_element_type=F32)
            ti_ref[...], tw_ref[...] = _top2(logits)

    if n_a == 1:
        compute(a_refs[0])
    else:
        pl.when(i < n_ctx)(lambda: compute(a_refs[0]))
        pl.when(i >= n_ctx)(lambda: compute(a_refs[1]))


def _down(a_list, w, layer, x, mods, *, mod_layer, bias=None, ln=None, gate_idx, norm_g, norm_layer, sc_idx, sh_idx,
          router_w=None, h_dtype=BF16, name):
    tm = 512
    n_ctx = T_CTX // tm
    kdim = w.shape[1]
    n_a = len(a_list)
    if n_a == 1:
        in_specs = [pl.BlockSpec((tm, kdim), lambda i: (i, 0))]
    else:
        in_specs = [pl.BlockSpec((tm, kdim), lambda i: (jnp.minimum(i, n_ctx - 1), 0)),
                    pl.BlockSpec((tm, kdim), lambda i: (jnp.maximum(i - n_ctx, 0), 0))]
    args = list(a_list)
    in_specs.append(pl.BlockSpec((None, kdim, D), lambda i: (layer, 0, 0), pipeline_mode=pl.Buffered(1)))
    args.append(w)
    row = lambda i: (0, 0)
    if bias is not None:
        in_specs.append(pl.BlockSpec((None, 1, D), lambda i: (layer, 0, 0)))
        args.append(bias.reshape(bias.shape[0], 1, D))
    if ln is not None:
        in_specs += [pl.BlockSpec((None, 1, D), lambda i: (layer, 0, 0))] * 2
        args += [ln[0].reshape(-1, 1, D), ln[1].reshape(-1, 1, D)]
    in_specs += [pl.BlockSpec((tm, D), lambda i: (i, 0)),
                 pl.BlockSpec((None, None, 6, D), lambda i: (mod_layer, _cond_row(i, tm), 0, 0)),
                 pl.BlockSpec((None, None, 6, D), lambda i: (norm_layer, _cond_row(i, tm), 0, 0)),
                 pl.BlockSpec((1, D), row)]
    args += [x, mods, mods, norm_g]
    out_shape = [SDS((T, D), F32), SDS((T, D), h_dtype)]
    out_specs = [pl.BlockSpec((tm, D), lambda i: (i, 0)), pl.BlockSpec((tm, D), lambda i: (i, 0))]
    if router_w is not None:
        in_specs.append(pl.BlockSpec((D, LANES), row))
        args.append(router_w)
        out_shape += [SDS((T, LANES), I32), SDS((T, LANES), F32)]
        out_specs += [pl.BlockSpec((tm, LANES), lambda i: (i, 0))] * 2
    return pl.pallas_call(
        functools.partial(_down_kernel, n_a=n_a, n_ctx=n_ctx, ln_prologue=ln is not None, has_bias=bias is not None,
                          gate_idx=gate_idx, sc_idx=sc_idx, sh_idx=sh_idx, router=router_w is not None),
        out_shape=tuple(out_shape), grid=(T // tm,), in_specs=in_specs, out_specs=tuple(out_specs),
        scratch_shapes=[pltpu.VMEM((kdim, D), BF16)],
        compiler_params=_cparams(1), name=name,
    )(*args)


TM_DISPATCH = 256


def _dispatch_kernel(pos_ref, lt_ref, ne_ref, h_ref, z_ref, xs_ref, sem, zsem):
    i = pl.program_id(0)

    @pl.when(i == 0)
    def _():
        for e in range(N_EXPERTS):
            @pl.when(ne_ref[e] > 0)
            def _(e=e):
                start = pl.multiple_of(lt_ref[e] * TM_MOE, TM_MOE)
                pltpu.make_async_copy(z_ref, xs_ref.at[pl.ds(start, TM_MOE)], zsem).start()
        for e in range(N_EXPERTS):
            @pl.when(ne_ref[e] > 0)
            def _():
                pltpu.make_async_copy(z_ref, xs_ref.at[pl.ds(0, TM_MOE)], zsem).wait()

    def row_copy(r, p):
        return pltpu.make_async_copy(h_ref.at[pl.ds(r, 1)], xs_ref.at[pl.ds(p, 1)], sem)

    def issue(r, c):
        t = i * TM_DISPATCH + r
        row_copy(r, pos_ref[2 * t]).start()
        row_copy(r, pos_ref[2 * t + 1]).start()
        return c

    lax.fori_loop(0, TM_DISPATCH, issue, 0)

    def wait_one(r, c):
        row_copy(0, 0).wait()
        return c

    lax.fori_loop(0, 2 * TM_DISPATCH, wait_one, 0)


def _dispatch(pos, last_tile, n_tiles_e, h, zeros_tile):
    return pl.pallas_call(
        _dispatch_kernel, out_shape=SDS((N_SLOTS, D), F32),
        grid_spec=pltpu.PrefetchScalarGridSpec(
            num_scalar_prefetch=3, grid=(T // TM_DISPATCH,),
            in_specs=[pl.BlockSpec((TM_DISPATCH, D), lambda i, p, lt, ne: (i, 0)),
                      pl.BlockSpec(memory_space=pl.ANY)],
            out_specs=pl.BlockSpec(memory_space=pl.ANY),
            scratch_shapes=[pltpu.SemaphoreType.DMA, pltpu.SemaphoreType.DMA]),
        compiler_params=_cparams(1), name="moe_dispatch",
    )(pos, last_tile, n_tiles_e, h, zeros_tile)


def _expert_changed(te_ref, t):
    return jnp.logical_or(t == 0, te_ref[t] != te_ref[jnp.maximum(t - 1, 0)])


def _moe_gu_kernel(te_ref, nt_ref, x_ref, wg_ref, wu_ref, o_ref, wgs, wus):
    t = pl.program_id(1)

    @pl.when(_expert_changed(te_ref, t))
    def _():
        wgs[...] = wg_ref[...].astype(BF16)
        wus[...] = wu_ref[...].astype(BF16)

    @pl.when(t < nt_ref[0])
    def _():
        x = x_ref[...].astype(BF16)
        g = jnp.dot(x, wgs[...], preferred_element_type=F32)
        u = jnp.dot(x, wus[...], preferred_element_type=F32)
        o_ref[...] = (_silu(g) * u).astype(o_ref.dtype)

    @pl.when(t >= nt_ref[0])
    def _():
        o_ref[...] = jnp.zeros_like(o_ref)


def _moe_gu(tile_expert, n_tiles, xs, w_gu, layer):
    tn = 896
    nj = E_FF // tn
    xmap = lambda j, t, te, nt: (jnp.minimum(t, nt[0] - 1), 0)
    return pl.pallas_call(
        _moe_gu_kernel, out_shape=SDS((N_SLOTS, E_FF), BF16),
        grid_spec=pltpu.PrefetchScalarGridSpec(
            num_scalar_prefetch=2, grid=(nj, NT_MOE),
            in_specs=[pl.BlockSpec((TM_MOE, D), xmap),
                      pl.BlockSpec((None, None, D, tn), lambda j, t, te, nt: (layer, te[t], 0, j)),
                      pl.BlockSpec((None, None, D, tn), lambda j, t, te, nt: (layer, te[t], 0, j + nj))],
            out_specs=pl.BlockSpec((TM_MOE, tn), lambda j, t, te, nt: (t, j)),
            scratch_shapes=[pltpu.VMEM((D, tn), BF16), pltpu.VMEM((D, tn), BF16)]),
        compiler_params=_cparams(2), name="moe_gate_up",
    )(tile_expert, n_tiles, xs, w_gu, w_gu)


def _moe_down_kernel(te_ref, nt_ref, h_ref, wd_ref, o_ref, wds):
    t = pl.program_id(1)

    @pl.when(_expert_changed(te_ref, t))
    def _():
        wds[...] = wd_ref[...].astype(BF16)

    @pl.when(t < nt_ref[0])
    def _():
        o_ref[...] = jnp.dot(h_ref[...], wds[...], preferred_element_type=F32)

    @pl.when(t >= nt_ref[0])
    def _():
        o_ref[...] = jnp.zeros_like(o_ref)


def _moe_down(tile_expert, n_tiles, hid, w_down, layer):
    tn = 512
    hmap = lambda j, t, te, nt: (jnp.minimum(t, nt[0] - 1), 0)
    return pl.pallas_call(
        _moe_down_kernel, out_shape=SDS((N_SLOTS, D), F32),
        grid_spec=pltpu.PrefetchScalarGridSpec(
            num_scalar_prefetch=2, grid=(D // tn, NT_MOE),
            in_specs=[pl.BlockSpec((TM_MOE, E_FF), hmap),
                      pl.BlockSpec((None, None, E_FF, tn), lambda j, t, te, nt: (layer, te[t], 0, j))],
            out_specs=pl.BlockSpec((TM_MOE, tn), lambda j, t, te, nt: (t, j)),
            scratch_shapes=[pltpu.VMEM((E_FF, tn), BF16)]),
        compiler_params=_cparams(2), name="moe_down",
    )(tile_expert, n_tiles, hid, w_down)


TM_COMBINE = 256


def _combine_kernel(pos_ref, y_ref, tw_ref, x_ref, m_ref, mn_ref, gn_ref, *rest, final, n_ctx):
    if final:
        yc_ref, ys_ref, buf0, buf1, sem = rest
    else:
        xo_ref, h_ref, buf0, buf1, sem = rest
    i = pl.program_id(0)

    def row_copy(p, buf, r):
        return pltpu.make_async_copy(y_ref.at[pl.ds(p, 1)], buf.at[pl.ds(r, 1)], sem)

    def issue(r, c):
        t = i * TM_COMBINE + r
        row_copy(pos_ref[2 * t], buf0, r).start()
        row_copy(pos_ref[2 * t + 1], buf1, r).start()
        return c

    lax.fori_loop(0, TM_COMBINE, issue, 0)

    def wait_one(r, c):
        row_copy(0, buf0, 0).wait()
        return c

    lax.fori_loop(0, 2 * TM_COMBINE, wait_one, 0)

    tw = tw_ref[...]
    f = tw[:, 0:1] * buf0[...] + tw[:, 1:2] * buf1[...]
    xn = x_ref[...] + m_ref[...][5:6] * f
    y = _rms(xn, gn_ref[...])
    if final:
        @pl.when(i < n_ctx)
        def _():
            yc_ref[...] = y

        @pl.when(i >= n_ctx)
        def _():
            ys_ref[...] = y
    else:
        mn = mn_ref[...]
        xo_ref[...] = xn
        h_ref[...] = (y * (1.0 + mn[1:2]) + mn[0:1]).astype(h_ref.dtype)


def _combine(pos, y, tw, x, mods, layer, norm_g, *, final):
    tm = TM_COMBINE
    n_ctx = T_CTX // tm
    norm_layer = layer if final else layer + 1
    tile = lambda i, p: (i, 0)
    in_specs = [pl.BlockSpec(memory_space=pl.ANY),
                pl.BlockSpec((tm, LANES), tile),
                pl.BlockSpec((tm, D), tile),
                pl.BlockSpec((None, None, 6, D), lambda i, p: (layer, _cond_row(i, tm), 0, 0)),
                pl.BlockSpec((None, None, 6, D), lambda i, p: (norm_layer, _cond_row(i, tm), 0, 0)),
                pl.BlockSpec((1, D), lambda i, p: (0, 0))]
    if final:
        out_shape = (SDS((T_CTX, D), F32), SDS((T_SMP, D), F32))
        out_specs = (pl.BlockSpec((tm, D), lambda i, p: (jnp.minimum(i, n_ctx - 1), 0)),
                     pl.BlockSpec((tm, D), lambda i, p: (jnp.maximum(i - n_ctx, 0), 0)))
    else:
        out_shape = (SDS((T, D), F32), SDS((T, D), BF16))
        out_specs = (pl.BlockSpec((tm, D), tile), pl.BlockSpec((tm, D), tile))
    return pl.pallas_call(
        functools.partial(_combine_kernel, final=final, n_ctx=n_ctx), out_shape=out_shape,
        grid_spec=pltpu.PrefetchScalarGridSpec(
            num_scalar_prefetch=1, grid=(T // tm,), in_specs=in_specs, out_specs=out_specs,
            scratch_shapes=[pltpu.VMEM((tm, D), F32), pltpu.VMEM((tm, D), F32), pltpu.SemaphoreType.DMA]),
        compiler_params=_cparams(1), name="moe_combine",
    )(pos, y, tw, x, mods, mods, norm_g)


def _route(top_i):
    i1, i2 = top_i[:, 0], top_i[:, 1]
    oh1 = jax.nn.one_hot(i1, N_EXPERTS, dtype=I32)
    oh2 = jax.nn.one_hot(i2, N_EXPERTS, dtype=I32)
    sel = oh1 + oh2
    csum = jnp.cumsum(sel, axis=0)
    rank = csum - sel
    counts = csum[-1]
    n_tiles_e = (counts + TM_MOE - 1) // TM_MOE
    tile_end = jnp.cumsum(n_tiles_e)
    tile_start = tile_end - n_tiles_e
    row_start = tile_start * TM_MOE
    pos1 = jnp.sum((row_start[None, :] + rank) * oh1, axis=1)
    pos2 = jnp.sum((row_start[None, :] + rank) * oh2, axis=1)
    pos = jnp.stack([pos1, pos2], axis=1).reshape(-1).astype(I32)
    total = tile_end[-1]
    tids = jnp.minimum(jnp.arange(NT_MOE, dtype=I32), total - 1)
    tile_expert = jnp.sum(tids[:, None] >= tile_end[None, :], axis=1).astype(I32)
    last_tile = (tile_end - 1).astype(I32)
    return pos, tile_expert, total.reshape(1).astype(I32), last_tile, n_tiles_e.astype(I32)


def _moe(h, top_i, top_w, x, mods, layer, j, w_gu, w_down, norm_g, *, final):
    pos, tile_expert, n_tiles, last_tile, n_tiles_e = _route(top_i)
    xs = _dispatch(pos, last_tile, n_tiles_e, h, jnp.zeros((TM_MOE, D), F32))
    hid = _moe_gu(tile_expert, n_tiles, xs, w_gu, j)
    y = _moe_down(tile_expert, n_tiles, hid, w_down, j)
    return _combine(pos, y, top_w, x, mods, layer, norm_g, final=final)


def kernel(x_prompt, x_sample, state_ret, c, c_ctx, w_ada, b_ada, norm_mix, norm_ffn, norm_final,
           conv_w_pw1, conv_b_pw1, conv_w_dw, conv_b_dw, conv_ln_g, conv_ln_b, conv_w_pw2, conv_b_pw2,
           ret_w_in, ret_decay_logit, ret_w_out, ffn_w_gu, ffn_w_down, moe_w_router, moe_w_gu, moe_w_down):
    cond = jnp.zeros((N_COND, D), F32).at[0].set(c_ctx).at[1:1 + B_SMP].set(c)
    mods = _ada(cond, w_ada, b_ada).reshape(DEPTH, N_COND, 6, D)
    log_g = jax.nn.log_sigmoid(ret_decay_logit.astype(F32))
    rope_tabs = _rope_tables()
    router_w = jnp.pad(moe_w_router, ((0, 0), (0, 0), (0, LANES - N_EXPERTS)))

    x, h = _norm0(x_prompt.reshape(T_CTX, D), x_sample.reshape(T_SMP, D), norm_mix[0:1], mods)
    states = []
    out = None
    for i in range(DEPTH):
        j = i // 2
        if i % 2 == 0:
            u = _mm_pair(h, conv_w_pw1, j, conv_b_pw1, D, D, "glu", F32)
            cv_c = _conv_ctx(u, conv_w_dw, conv_b_dw, j)
            cv_s = _conv_smp(u, conv_w_dw, conv_b_dw, j)
            x, h = _down([cv_c, cv_s], conv_w_pw2, j, x, mods, mod_layer=i, bias=conv_b_pw2, ln=(conv_ln_g, conv_ln_b),
                         gate_idx=2, norm_g=norm_ffn[i:i + 1], norm_layer=i, sc_idx=4, sh_idx=3, name="down_conv")
            hid = _mm_pair(h, ffn_w_gu, j, None, D_FF, D_FF // 2, "swiglu", BF16)
            x, h = _down([hid], ffn_w_down, j, x, mods, mod_layer=i, gate_idx=5, norm_g=norm_mix[i + 1:i + 2],
                         norm_layer=i + 1, sc_idx=1, sh_idx=0, name="down_ffn")
        else:
            qkv = _mm(h, ret_w_in, j, RET_IN, BF16)
            y_c, s_fin = _retention(qkv, log_g[j], ctx=True)
            (y_s,) = _retention(qkv, log_g[j], ctx=False, rope_tabs=rope_tabs, s0=state_ret[:, j])
            states.append(s_fin)
            x, h, top_i, top_w = _down([y_c, y_s], ret_w_out, j, x, mods, mod_layer=i, gate_idx=2, norm_g=norm_ffn[i:i + 1],
                                       norm_layer=i, sc_idx=4, sh_idx=3, router_w=router_w[j], h_dtype=F32,
                                       name="down_ret")
            final = i == DEPTH - 1
            res = _moe(h, top_i, top_w, x, mods, i, j, moe_w_gu, moe_w_down,
                       norm_final.reshape(1, D) if final else norm_mix[i + 1:i + 2], final=final)
            if final:
                out = res
            else:
                x, h = res
    y_prompt = out[0].reshape(B_CTX, L_CTX, D)
    y_sample = out[1].reshape(B_SMP, L_SMP, D)
    state_new = jnp.stack(states, axis=1).astype(x_prompt.dtype)
    return (y_prompt, y_sample, state_new)
```
